```python
import math, functools
import jax, jax.numpy as jnp
from jax import lax
import numpy as np

D_MODEL = 1024
BATCH = 8
SEQ = 2048
DEPTH = 2
DEC_BATCH = 128
DEC_SEQ = 4
PAST_LEN = 2048
PAGE_SIZE = 128

HEAD_DIM = 64
N_SB_HEADS = 8
N_DSA_HEADS = 8
N_IDX_HEADS = 8
IDX_DIM = 64
TOPK_MAX = 256
N_DIFF_HEADS = 8
W_SB = N_SB_HEADS * HEAD_DIM
W_DSA = N_DSA_HEADS * HEAD_DIM
W_AB = W_SB + W_DSA
W_IDXQ = N_IDX_HEADS * IDX_DIM
AB_SPLIT = (W_SB, W_SB, W_SB, W_DSA, W_DSA, W_DSA, W_IDXQ, IDX_DIM, N_IDX_HEADS, W_AB)
AB_COLS = 3 * W_SB + 3 * W_DSA + W_IDXQ + IDX_DIM + N_IDX_HEADS + W_AB
W_C = N_DIFF_HEADS * 2 * HEAD_DIM
C_SPLIT = (W_C, W_C, W_C, W_C)
C_COLS = 4 * W_C
ROPE_THETA = 10000.0
NORM_EPS = 1e-6
SUBLN_EPS = 1e-5
Q_BLOCK = 128
ATTN_SCALE = 1.0 / math.sqrt(HEAD_DIM)
N_AB_LAYERS = (DEPTH + 1) // 2
N_C_LAYERS = DEPTH // 2

kernel_name = "stickbreak_dsa_diffattn_hybrid_step"


def split_cols(y, sizes):
    offs = np.cumsum(sizes)[:-1].tolist()
    return jnp.split(y, offs, axis=-1)


def rmsnorm(x, g, eps=NORM_EPS):
    xf = x.astype(jnp.float32)
    y = xf * lax.rsqrt(jnp.mean(xf * xf, axis=-1, keepdims=True) + eps)
    return (y * g.astype(jnp.float32)).astype(x.dtype)


def rope(x, pos):
    half = x.shape[-1] // 2
    inv_freq = ROPE_THETA ** (-jnp.arange(half, dtype=jnp.float32) / half)
    ang = pos.astype(jnp.float32)[:, None] * inv_freq[None, :]
    shape = (pos.shape[0],) + (1,) * (x.ndim - 3) + (half,)
    cos = jnp.cos(ang).reshape(shape)
    sin = jnp.sin(ang).reshape(shape)
    xf = x.astype(jnp.float32)
    x1, x2 = xf[..., :half], xf[..., half:]
    return jnp.concatenate([x1 * cos - x2 * sin, x2 * cos + x1 * sin], axis=-1).astype(x.dtype)


def sb_core(q_pos, qs, kvs):
    (q,) = qs
    (kv,) = kvs
    k, v = kv[:, 0], kv[:, 1]
    L = k.shape[0]
    z = jnp.einsum('qhd,lhd->hql', q, k).astype(jnp.float32) * ATTN_SCALE
    before = jnp.arange(L)[None, :] < q_pos[:, None]
    log_fail = jnp.where(before, jax.nn.log_sigmoid(-z), 0.0)
    later = lax.cumsum(log_fail, axis=2, reverse=True) - log_fail
    a = jnp.where(before, jnp.exp(jax.nn.log_sigmoid(z) + later), 0.0)
    return jnp.einsum('hql,lhd->qhd', a.astype(v.dtype), v)


def dsa_core(q_pos, qs, kvs):
    q, q_idx, w_idx = qs
    kv, k_idx = kvs
    L = kv.shape[0]
    topk = min(TOPK_MAX, L // 4)
    visible = jnp.arange(L)[None, :] <= q_pos[:, None]
    dots = jnp.einsum('qhe,le->qhl', q_idx, k_idx).astype(jnp.float32)
    score = jnp.einsum('qh,qhl->ql', w_idx.astype(jnp.float32), jax.nn.relu(dots))
    score = jnp.where(visible, score, -jnp.inf)
    _, sel = lax.top_k(score, topk)
    k_sel = kv[:, 0][sel]
    v_sel = kv[:, 1][sel]
    logits = jnp.einsum('qhd,qkhd->qhk', q, k_sel).astype(jnp.float32) * ATTN_SCALE
    ok = sel <= q_pos[:, None]
    logits = jnp.where(ok[:, None, :], logits, -jnp.inf)
    p = jax.nn.softmax(logits, axis=-1)
    return jnp.einsum('qhk,qkhd->qhd', p.astype(v_sel.dtype), v_sel)


def diff_core(lam, q_pos, qs, kvs):
    (q,) = qs
    (kv,) = kvs
    L = kv.shape[0]
    k = kv[:, 0].reshape(L, N_DIFF_HEADS, 2, HEAD_DIM)
    v = kv[:, 1]
    logits = jnp.einsum('qhcd,lhcd->chql', q, k).astype(jnp.float32) * ATTN_SCALE
    causal = jnp.arange(L)[None, :] <= q_pos[:, None]
    p = jax.nn.softmax(jnp.where(causal, logits, -jnp.inf), axis=-1)
    a = p[0] - lam * p[1]
    return jnp.einsum('hql,lhe->qhe', a.astype(v.dtype), v)


def prompt_sweep(core, qs, kvs):
    B, S = qs[0].shape[:2]
    nb = S // Q_BLOCK

    def body(i):
        b = i // nb
        start = (i % nb) * Q_BLOCK
        q_blk = tuple(lax.dynamic_slice_in_dim(a[b], start, Q_BLOCK, axis=0) for a in qs)
        kv_seq = tuple(a[b] for a in kvs)
        return core(start + jnp.arange(Q_BLOCK), q_blk, kv_seq)

    out = lax.map(body, jnp.arange(B * nb))
    return out.reshape((B, S) + out.shape[2:])


def sample_sweep(core, qs, kvs, pools, layer, page_table):
    T = qs[0].shape[1]
    past = page_table.shape[1] * pools[0].shape[2]
    q_pos = past + jnp.arange(T)

    def body(b):
        pages = page_table[b]
        kv_seq = tuple(
            jnp.concatenate([pool[layer, pages].reshape((-1,) + pool.shape[3:]), new[b]], axis=0)
            for pool, new in zip(pools, kvs))
        return core(q_pos, tuple(a[b] for a in qs), kv_seq)

    return lax.map(body, jnp.arange(qs[0].shape[0]))


def ab_layer(h, pos, w_in, w_out, sweep_sb, sweep_dsa):
    N, T, _ = h.shape
    y = h @ w_in
    q_sb, k_sb, v_sb, q_d, k_d, v_d, q_i, k_i, w_i, gate = split_cols(y, AB_SPLIT)
    q_sb = q_sb.reshape(N, T, N_SB_HEADS, HEAD_DIM)
    kv_sb = jnp.stack([k_sb.reshape(N, T, N_SB_HEADS, HEAD_DIM),
                       v_sb.reshape(N, T, N_SB_HEADS, HEAD_DIM)], axis=2)
    q_d = rope(q_d.reshape(N, T, N_DSA_HEADS, HEAD_DIM), pos)
    kv_dsa = jnp.stack([rope(k_d.reshape(N, T, N_DSA_HEADS, HEAD_DIM), pos),
                        v_d.reshape(N, T, N_DSA_HEADS, HEAD_DIM)], axis=2)
    q_i = rope(q_i.reshape(N, T, N_IDX_HEADS, IDX_DIM), pos)
    k_i = rope(k_i[:, :, None, :], pos)[:, :, 0]
    w_i = w_i * (N_IDX_HEADS ** -0.5)
    o_sb = sweep_sb(sb_core, (q_sb,), (kv_sb,))
    o_dsa = sweep_dsa(dsa_core, (q_d, q_i, w_i), (kv_dsa, k_i))
    o = jnp.concatenate([o_sb.reshape(N, T, W_SB), o_dsa.reshape(N, T, W_DSA)], axis=-1)
    out = (o * jax.nn.silu(gate)) @ w_out
    return out, kv_sb, kv_dsa, k_i


def c_layer(h, pos, w_in, w_out, lam_params, subln_g, lambda_init, sweep):
    N, T, _ = h.shape
    y = h @ w_in
    q, k, v, gate = split_cols(y, C_SPLIT)
    q = rope(q.reshape(N, T, N_DIFF_HEADS, 2, HEAD_DIM), pos)
    k = rope(k.reshape(N, T, N_DIFF_HEADS, 2, HEAD_DIM), pos).reshape(N, T, N_DIFF_HEADS, 2 * HEAD_DIM)
    kv = jnp.stack([k, v.reshape(N, T, N_DIFF_HEADS, 2 * HEAD_DIM)], axis=2)
    lp = lam_params.astype(jnp.float32)
    lam = jnp.exp(jnp.sum(lp[0] * lp[1])) - jnp.exp(jnp.sum(lp[2] * lp[3])) + lambda_init
    o = sweep(functools.partial(diff_core, lam), (q,), (kv,))
    o = rmsnorm(o, subln_g, SUBLN_EPS) * (1.0 - lambda_init)
    out = (o.reshape(N, T, W_C) * jax.nn.silu(gate)) @ w_out
    return out, kv


def setup_inputs(seed: int = 0) -> dict:
    key = jax.random.key(seed)
    ks = jax.random.split(key, 16)
    n_pages = PAST_LEN // PAGE_SIZE
    n_used = DEC_BATCH * n_pages
    n_pool = n_used + n_used // 4
    perm = jax.random.permutation(ks[0], n_pool)
    page_table = perm[:n_used].reshape(DEC_BATCH, n_pages).astype(jnp.int32)
    f32 = jnp.float32
    nrm = lambda k, s: jax.random.normal(k, s, dtype=f32)
    return {
        "x_prompt": nrm(ks[1], (BATCH, SEQ, D_MODEL)),
        "x_sample": nrm(ks[2], (DEC_BATCH, DEC_SEQ, D_MODEL)),
        "cache_kv_sb": nrm(ks[3], (N_AB_LAYERS, n_pool, PAGE_SIZE, 2, N_SB_HEADS, HEAD_DIM)),
        "cache_kv_dsa": nrm(ks[4], (N_AB_LAYERS, n_pool, PAGE_SIZE, 2, N_DSA_HEADS, HEAD_DIM)),
        "cache_kidx_dsa": nrm(ks[5], (N_AB_LAYERS, n_pool, PAGE_SIZE, IDX_DIM)),
        "cache_kv_diff": nrm(ks[6], (N_C_LAYERS, n_pool, PAGE_SIZE, 2, N_DIFF_HEADS, 2 * HEAD_DIM)),
        "page_table": page_table,
        "norm_g": 1.0 + 0.02 * nrm(ks[7], (DEPTH, D_MODEL)),
        "w_in_ab": nrm(ks[8], (N_AB_LAYERS, D_MODEL, AB_COLS)) * D_MODEL ** -0.5,
        "w_out_ab": nrm(ks[9], (N_AB_LAYERS, W_AB, D_MODEL)) * W_AB ** -0.5,
        "w_in_c": nrm(ks[10], (N_C_LAYERS, D_MODEL, C_COLS)) * D_MODEL ** -0.5,
        "w_out_c": nrm(ks[11], (N_C_LAYERS, W_C, D_MODEL)) * W_C ** -0.5,
        "diff_lambda": 0.1 * nrm(ks[12], (N_C_LAYERS, 4, HEAD_DIM)),
        "diff_subln_g": 1.0 + 0.02 * nrm(ks[13], (N_C_LAYERS, 2 * HEAD_DIM)),
        "final_norm_g": 1.0 + 0.02 * nrm(ks[14], (D_MODEL,)),
    }


def reference(x_prompt, x_sample, cache_kv_sb, cache_kv_dsa, cache_kidx_dsa, cache_kv_diff, page_table,
              norm_g, w_in_ab, w_out_ab, w_in_c, w_out_c, diff_lambda, diff_subln_g, final_norm_g):
    S = x_prompt.shape[1]
    T = x_sample.shape[1]
    past = page_table.shape[1] * cache_kv_sb.shape[2]
    pos_p = jnp.arange(S)
    pos_s = past + jnp.arange(T)
    xp, xs = x_prompt, x_sample
    sb_p, sb_s, dsa_p, dsa_s, idx_p, idx_s, dif_p, dif_s = [], [], [], [], [], [], [], []
    for layer in range(DEPTH):
        i = layer // 2
        hp = rmsnorm(xp, norm_g[layer])
        hs = rmsnorm(xs, norm_g[layer])
        if layer % 2 == 0:
            out_p, kvsb, kvd, ki = ab_layer(hp, pos_p, w_in_ab[i], w_out_ab[i], prompt_sweep, prompt_sweep)
            sb_p.append(kvsb); dsa_p.append(kvd); idx_p.append(ki)
            sweep_sb = functools.partial(sample_sweep, pools=(cache_kv_sb,), layer=i, page_table=page_table)
            sweep_dsa = functools.partial(sample_sweep, pools=(cache_kv_dsa, cache_kidx_dsa), layer=i,
                                          page_table=page_table)
            out_s, kvsb, kvd, ki = ab_layer(hs, pos_s, w_in_ab[i], w_out_ab[i], sweep_sb, sweep_dsa)
            sb_s.append(kvsb); dsa_s.append(kvd); idx_s.append(ki)
        else:
            lambda_init = 0.8 - 0.6 * math.exp(-0.3 * layer)
            out_p, kvc = c_layer(hp, pos_p, w_in_c[i], w_out_c[i], diff_lambda[i], diff_subln_g[i],
                                 lambda_init, prompt_sweep)
            dif_p.append(kvc)
            sweep_c = functools.partial(sample_sweep, pools=(cache_kv_diff,), layer=i, page_table=page_table)
            out_s, kvc = c_layer(hs, pos_s, w_in_c[i], w_out_c[i], diff_lambda[i], diff_subln_g[i],
                                 lambda_init, sweep_c)
            dif_s.append(kvc)
        xp = xp + out_p
        xs = xs + out_s
    y_prompt = rmsnorm(xp, final_norm_g)
    y_sample = rmsnorm(xs, final_norm_g)
    kv_sb_prompt = jnp.stack(sb_p)
    kv_sb_sample = jnp.stack(sb_s)
    kv_dsa_prompt = jnp.stack(dsa_p)
    kv_dsa_sample = jnp.stack(dsa_s)
    kidx_prompt = jnp.stack(idx_p)
    kidx_sample = jnp.stack(idx_s)
    kv_diff_prompt = jnp.stack(dif_p)
    kv_diff_sample = jnp.stack(dif_s)
    return (y_prompt, y_sample, kv_sb_prompt, kv_sb_sample, kv_dsa_prompt, kv_dsa_sample,
            kidx_prompt, kidx_sample, kv_diff_prompt, kv_diff_sample)
```

```python
import functools
import math

import jax
import jax.numpy as jnp
from jax import lax
from jax.experimental import pallas as pl
from jax.experimental.pallas import tpu as pltpu

F32 = jnp.float32
MXU_DTYPE = jnp.bfloat16

HEAD_DIM = 64
N_HEADS = 8
IDX_DIM = 64
TOPK_MAX = 256
ROPE_THETA = 10000.0
NORM_EPS = 1e-6
SUBLN_EPS = 1e-5
ATTN_SCALE = 1.0 / math.sqrt(HEAD_DIM)
W_HEADS = N_HEADS * HEAD_DIM
W_DIFF = 2 * W_HEADS

LANES = 128
TQ = 256
CK = 128
TM_PROJ = 256
TM_OUT = 256
NEG_BIG = -1e30
SB_UNDERFLOW = -104.0
INT_MIN = -(2 ** 31)
VMEM_LIMIT = 56 * 1024 * 1024


def _cparams(n_axes, vmem=VMEM_LIMIT):
    return pltpu.CompilerParams(dimension_semantics=("arbitrary",) * n_axes, vmem_limit_bytes=vmem)


def _dot(a, b):
    return jnp.dot(a, b, preferred_element_type=F32)


def _dot_nt(a, b):
    return lax.dot_general(a, b, (((1,), (1,)), ((), ())), preferred_element_type=F32)


def _silu(g):
    return g / (1.0 + jnp.exp(-g))


def _rope_tables(pos):
    half = HEAD_DIM // 2
    inv_freq = ROPE_THETA ** (-jnp.arange(half, dtype=F32) / half)
    ang = pos.astype(F32)[:, None] * inv_freq[None, :]
    cos, sin = jnp.cos(ang), jnp.sin(ang)
    cos64 = jnp.concatenate([cos, cos], axis=-1)
    sin64 = jnp.concatenate([-sin, sin], axis=-1)
    n = pos.shape[0]
    cos_h = jnp.concatenate([cos64, cos64], axis=-1)
    sin_h = jnp.concatenate([sin64, sin64], axis=-1)
    wscale = jnp.full((n, N_HEADS), N_HEADS ** -0.5, F32)
    cos_m = jnp.concatenate([cos64, wscale, jnp.zeros((n, LANES - IDX_DIM - N_HEADS), F32)], axis=-1)
    sin_m = jnp.concatenate([sin64, jnp.zeros((n, LANES - IDX_DIM), F32)], axis=-1)
    return cos_h, sin_h, cos_m, sin_m


def _rope(y, cos, sin):
    lane = lax.broadcasted_iota(jnp.int32, cos.shape, 1)
    first_half = (lane & (HEAD_DIM // 2)) == 0
    outs = []
    for c in range(y.shape[1] // LANES):
        yc = y[:, c * LANES:(c + 1) * LANES]
        partner = jnp.where(first_half, pltpu.roll(yc, LANES - HEAD_DIM // 2, 1),
                            pltpu.roll(yc, HEAD_DIM // 2, 1))
        outs.append(yc * cos + partner * sin)
    return outs[0] if len(outs) == 1 else jnp.concatenate(outs, axis=1)


def _norm_rows(x_ref, g_ref, h_ref):
    x = x_ref[...]
    ms = jnp.mean(x * x, axis=-1, keepdims=True)
    h_ref[...] = (x * lax.rsqrt(ms + NORM_EPS) * g_ref[...]).astype(h_ref.dtype)


AB_QSB, AB_KSB, AB_VSB, AB_QD, AB_KD, AB_VD, AB_QI = (i * W_HEADS for i in range(7))
AB_MISC = 7 * W_HEADS
AB_GATE = AB_MISC + LANES
AB_COLS_PACKED = AB_GATE + 2 * W_HEADS


def _pack_w_in_ab(w):
    d = w.shape[0]
    main = w[:, :7 * W_HEADS]
    misc = w[:, 7 * W_HEADS:7 * W_HEADS + IDX_DIM + N_HEADS]
    gate = w[:, 7 * W_HEADS + IDX_DIM + N_HEADS:]
    pad = jnp.zeros((d, LANES - IDX_DIM - N_HEADS), w.dtype)
    return jnp.concatenate([main, misc, pad, gate], axis=1).astype(MXU_DTYPE)


def _proj_ab_kernel(x_ref, g_ref, w_ref, cos_ref, sin_ref, cosm_ref, sinm_ref, *refs, transposed):
    h_ref = refs[-1]
    outs = refs[:-1]
    _norm_rows(x_ref, g_ref, h_ref)
    cos, sin = cos_ref[...], sin_ref[...]

    def grp(c0, width):
        return _dot(h_ref[...], w_ref[:, c0:c0 + width])

    q_sb = grp(AB_QSB, W_HEADS) * ATTN_SCALE
    k_sb = grp(AB_KSB, W_HEADS)
    v_sb = grp(AB_VSB, W_HEADS)
    q_d = _rope(grp(AB_QD, W_HEADS), cos, sin) * ATTN_SCALE
    k_d = _rope(grp(AB_KD, W_HEADS), cos, sin)
    v_d = grp(AB_VD, W_HEADS)
    q_i = _rope(grp(AB_QI, W_HEADS), cos, sin)
    misc = _rope(grp(AB_MISC, LANES), cosm_ref[...], sinm_ref[...])
    gate = grp(AB_GATE, 2 * W_HEADS)

    if transposed:
        (kv_sb_o, kv_d_o, misc_o, qT_sb_o, k_sb_o, vT_sb_o, qT_d_o, k_d_o, vT_d_o,
         qT_i_o, k_i_o, wT_o, gate_o) = outs
    else:
        kv_sb_o, kv_d_o, misc_o, q_sb_o, q_d_o, q_i_o, gate_o = outs
    kv_sb_o[:, :W_HEADS] = k_sb
    kv_sb_o[:, W_HEADS:] = v_sb
    kv_d_o[:, :W_HEADS] = k_d
    kv_d_o[:, W_HEADS:] = v_d
    misc_o[...] = misc
    gate_o[...] = gate.astype(gate_o.dtype)
    if transposed:
        qT_sb_o[...] = q_sb.T.astype(qT_sb_o.dtype)
        qT_d_o[...] = q_d.T.astype(qT_d_o.dtype)
        qT_i_o[...] = q_i.T.astype(qT_i_o.dtype)
        k_sb_o[...] = k_sb.astype(k_sb_o.dtype)
        k_d_o[...] = k_d.astype(k_d_o.dtype)
        k_i_o[...] = misc[:, :IDX_DIM].astype(k_i_o.dtype)
        vT_sb = v_sb.T
        vT_d = v_d.T
        for c in range(vT_sb_o.shape[0]):
            vT_sb_o[c] = vT_sb[:, c * CK:(c + 1) * CK].astype(vT_sb_o.dtype)
            vT_d_o[c] = vT_d[:, c * CK:(c + 1) * CK].astype(vT_d_o.dtype)
        wT_o[...] = misc.T[IDX_DIM:IDX_DIM + N_HEADS, :]
    else:
        q_sb_o[...] = q_sb.astype(q_sb_o.dtype)
        q_d_o[...] = q_d.astype(q_d_o.dtype)
        q_i_o[...] = q_i.astype(q_i_o.dtype)


def _proj_ab(x, g, w, tables, *, rows_per_seq, transposed):
    r, d = x.shape
    tm = TM_PROJ
    nt = r // tm
    tps = rows_per_seq // tm
    row = lambda i: (i, 0)
    tab = pl.BlockSpec((tm, LANES), lambda i: (i % tps, 0))
    in_specs = [pl.BlockSpec((tm, d), row), pl.BlockSpec((1, d), lambda i: (0, 0)),
                pl.BlockSpec((d, AB_COLS_PACKED), lambda i: (0, 0)), tab, tab, tab, tab]
    f32_outs = [jax.ShapeDtypeStruct((r, 2 * W_HEADS), F32), jax.ShapeDtypeStruct((r, 2 * W_HEADS), F32),
                jax.ShapeDtypeStruct((r, LANES), F32)]
    f32_specs = [pl.BlockSpec((tm, 2 * W_HEADS), row), pl.BlockSpec((tm, 2 * W_HEADS), row),
                 pl.BlockSpec((tm, LANES), row)]
    gate_out = jax.ShapeDtypeStruct((r, 2 * W_HEADS), MXU_DTYPE)
    gate_spec = pl.BlockSpec((tm, 2 * W_HEADS), row)
    if transposed:
        assert tm == TQ
        qT = jax.ShapeDtypeStruct((nt, W_HEADS, tm), MXU_DTYPE)
        qT_spec = pl.BlockSpec((None, W_HEADS, tm), lambda i: (i, 0, 0))
        kb = jax.ShapeDtypeStruct((r, W_HEADS), MXU_DTYPE)
        kb_spec = pl.BlockSpec((tm, W_HEADS), row)
        vT = jax.ShapeDtypeStruct((r // CK, W_HEADS, CK), MXU_DTYPE)
        vT_spec = pl.BlockSpec((tm // CK, W_HEADS, CK), lambda i: (i, 0, 0))
        out_shape = f32_outs + [qT, kb, vT, qT, kb, vT, qT,
                                jax.ShapeDtypeStruct((r, IDX_DIM), MXU_DTYPE),
                                jax.ShapeDtypeStruct((nt, N_HEADS, tm), F32), gate_out]
        out_specs = f32_specs + [qT_spec, kb_spec, vT_spec, qT_spec, kb_spec, vT_spec, qT_spec,
                                 pl.BlockSpec((tm, IDX_DIM), row),
                                 pl.BlockSpec((None, N_HEADS, tm), lambda i: (i, 0, 0)), gate_spec]
    else:
        qb = jax.ShapeDtypeStruct((r, W_HEADS), MXU_DTYPE)
        qb_spec = pl.BlockSpec((tm, W_HEADS), row)
        out_shape = f32_outs + [qb, qb, qb, gate_out]
        out_specs = f32_specs + [qb_spec, qb_spec, qb_spec, gate_spec]
    return pl.pallas_call(
        functools.partial(_proj_ab_kernel, transposed=transposed),
        grid=(nt,), in_specs=in_specs, out_specs=out_specs, out_shape=out_shape,
        scratch_shapes=[pltpu.VMEM((tm, d), MXU_DTYPE)],
        compiler_params=_cparams(1), name="proj_ab_t" if transposed else "proj_ab_s",
    )(x, g, w, *tables)


def _proj_c_kernel(x_ref, g_ref, w_ref, cos_ref, sin_ref, *refs, transposed):
    h_ref = refs[-1]
    outs = refs[:-1]
    _norm_rows(x_ref, g_ref, h_ref)
    cos, sin = cos_ref[...], sin_ref[...]

    def grp(c0):
        return _dot(h_ref[...], w_ref[:, c0:c0 + W_DIFF])

    q = _rope(grp(0), cos, sin) * ATTN_SCALE
    k = _rope(grp(W_DIFF), cos, sin)
    v = grp(2 * W_DIFF)
    gate = grp(3 * W_DIFF)
    if transposed:
        kv_o, qT_o, k_o, vT_o, gate_o = outs
    else:
        kv_o, q_o, gate_o = outs
    kv_o[:, :W_DIFF] = k
    kv_o[:, W_DIFF:] = v
    gate_o[...] = gate.astype(gate_o.dtype)
    if transposed:
        qT_o[...] = q.T.astype(qT_o.dtype)
        k_o[...] = k.astype(k_o.dtype)
        vT = v.T
        for c in range(vT_o.shape[0]):
            vT_o[c] = vT[:, c * CK:(c + 1) * CK].astype(vT_o.dtype)
    else:
        q_o[...] = q.astype(q_o.dtype)


def _proj_c(x, g, w, tables, *, rows_per_seq, transposed):
    r, d = x.shape
    tm = TM_PROJ
    nt = r // tm
    tps = rows_per_seq // tm
    row = lambda i: (i, 0)
    tab = pl.BlockSpec((tm, LANES), lambda i: (i % tps, 0))
    in_specs = [pl.BlockSpec((tm, d), row), pl.BlockSpec((1, d), lambda i: (0, 0)),
                pl.BlockSpec((d, 4 * W_DIFF), lambda i: (0, 0)), tab, tab]
    kv = jax.ShapeDtypeStruct((r, 2 * W_DIFF), F32)
    kv_spec = pl.BlockSpec((tm, 2 * W_DIFF), row)
    gate_out = jax.ShapeDtypeStruct((r, W_DIFF), MXU_DTYPE)
    gate_spec = pl.BlockSpec((tm, W_DIFF), row)
    if transposed:
        out_shape = [kv, jax.ShapeDtypeStruct((nt, W_DIFF, tm), MXU_DTYPE),
                     jax.ShapeDtypeStruct((r, W_DIFF), MXU_DTYPE),
                     jax.ShapeDtypeStruct((r // CK, W_DIFF, CK), MXU_DTYPE), gate_out]
        out_specs = [kv_spec, pl.BlockSpec((None, W_DIFF, tm), lambda i: (i, 0, 0)),
                     pl.BlockSpec((tm, W_DIFF), row),
                     pl.BlockSpec((tm // CK, W_DIFF, CK), lambda i: (i, 0, 0)), gate_spec]
    else:
        out_shape = [kv, jax.ShapeDtypeStruct((r, W_DIFF), MXU_DTYPE), gate_out]
        out_specs = [kv_spec, pl.BlockSpec((tm, W_DIFF), row), gate_spec]
    return pl.pallas_call(
        functools.partial(_proj_c_kernel, transposed=transposed),
        grid=(nt,), in_specs=in_specs, out_specs=out_specs, out_shape=out_shape,
        scratch_shapes=[pltpu.VMEM((tm, d), MXU_DTYPE)],
        compiler_params=_cparams(1), name="proj_c_t" if transposed else "proj_c_s",
    )(x, g, w, *tables[:2])


def _out_ab_kernel(x_ref, a_ref, b_ref, w_ref, o_ref):
    o_ref[...] = (x_ref[...] + _dot(a_ref[...], w_ref[:W_HEADS, :]) + _dot(b_ref[...], w_ref[W_HEADS:, :]))


def _out_ab(x, og_a, og_b, w):
    r, d = x.shape
    tm = TM_OUT
    row = lambda i: (i, 0)
    return pl.pallas_call(
        _out_ab_kernel, grid=(r // tm,),
        in_specs=[pl.BlockSpec((tm, d), row), pl.BlockSpec((tm, W_HEADS), row),
                  pl.BlockSpec((tm, W_HEADS), row), pl.BlockSpec((2 * W_HEADS, d), lambda i: (0, 0))],
        out_specs=pl.BlockSpec((tm, d), row), out_shape=jax.ShapeDtypeStruct((r, d), F32),
        compiler_params=_cparams(1), name="out_ab",
    )(x, og_a, og_b, w)


def _out_c_kernel(x_ref, a_ref, w_ref, g_ref, o_ref, *, final_norm):
    xn = x_ref[...] + _dot(a_ref[...], w_ref[...])
    if final_norm:
        ms = jnp.mean(xn * xn, axis=-1, keepdims=True)
        xn = xn * lax.rsqrt(ms + NORM_EPS) * g_ref[...]
    o_ref[...] = xn


def _out_c(x, og, w, g, *, final_norm):
    r, d = x.shape
    tm = TM_OUT
    row = lambda i: (i, 0)
    return pl.pallas_call(
        functools.partial(_out_c_kernel, final_norm=final_norm), grid=(r // tm,),
        in_specs=[pl.BlockSpec((tm, d), row), pl.BlockSpec((tm, W_DIFF), row),
                  pl.BlockSpec((W_DIFF, d), lambda i: (0, 0)), pl.BlockSpec((1, d), lambda i: (0, 0))],
        out_specs=pl.BlockSpec((tm, d), row), out_shape=jax.ShapeDtypeStruct((r, d), F32),
        compiler_params=_cparams(1), name="out_c",
    )(x, og, w, g)


def _final_norm_kernel(x_ref, g_ref, o_ref):
    x = x_ref[...]
    ms = jnp.mean(x * x, axis=-1, keepdims=True)
    o_ref[...] = x * lax.rsqrt(ms + NORM_EPS) * g_ref[...]


def _final_norm(x, g):
    r, d = x.shape
    tm = TM_OUT
    row = lambda i: (i, 0)
    return pl.pallas_call(
        _final_norm_kernel, grid=(r // tm,),
        in_specs=[pl.BlockSpec((tm, d), row), pl.BlockSpec((1, d), lambda i: (0, 0))],
        out_specs=pl.BlockSpec((tm, d), row), out_shape=jax.ShapeDtypeStruct((r, d), F32),
        compiler_params=_cparams(1), name="final_norm",
    )(x, g)


def _split_head_pairs(qT_ref, qz_ref):
    n_pairs = qT_ref.shape[0] // LANES
    row = lax.broadcasted_iota(jnp.int32, (LANES, qT_ref.shape[1]), 0)
    for p in range(n_pairs):
        qp = qT_ref[p * LANES:(p + 1) * LANES, :]
        zero = jnp.zeros_like(qp)
        qz_ref[2 * p] = jnp.where(row < HEAD_DIM, qp, zero)
        qz_ref[2 * p + 1] = jnp.where(row >= HEAD_DIM, qp, zero)


def _key_chunk(ref, c, lane0, width):
    return ref[pl.ds(pl.multiple_of(c * CK, CK), CK), lane0:lane0 + width]


def _sb_prompt_kernel(qT_ref, k_ref, vT_ref, gate_ref, o_ref, qz_ref, carry_ref, acc_ref):
    i = pl.program_id(1)
    _split_head_pairs(qT_ref, qz_ref)
    carry_ref[...] = jnp.zeros_like(carry_ref)
    acc_ref[...] = jnp.zeros_like(acc_ref)
    r_io = lax.broadcasted_iota(jnp.int32, (CK, TQ), 0)
    c_io = lax.broadcasted_iota(jnp.int32, (CK, TQ), 1)
    u_r = lax.broadcasted_iota(jnp.int32, (CK, 2 * CK), 0)
    u_c = lax.broadcasted_iota(jnp.int32, (CK, 2 * CK), 1)
    ut2 = jnp.where((u_c & (CK - 1)) > u_r, 1.0, 0.0).astype(MXU_DTYPE)
    q0 = i * TQ

    def body(state):
        j, _ = state
        before = (j * CK + r_io) < (q0 + c_io)
        for h in range(N_HEADS):
            kp = _key_chunk(k_ref, j, (h // 2) * LANES, LANES)
            z = _dot(kp, qz_ref[h])
            ls_pos = jnp.minimum(z, 0.0) - jnp.log(1.0 + jnp.exp(-jnp.abs(z)))
            lf = jnp.where(before, ls_pos - z, 0.0)
            hi = lf.astype(MXU_DTYPE)
            lo = (lf - hi.astype(F32)).astype(MXU_DTYPE)
            later = _dot(ut2, jnp.concatenate([hi, lo], axis=0))
            c = carry_ref[h:h + 1, :]
            a = jnp.where(before, jnp.exp(ls_pos + later + c), 0.0)
            vt = vT_ref[j, h * HEAD_DIM:(h + 1) * HEAD_DIM, :]
            acc_ref[h * HEAD_DIM:(h + 1) * HEAD_DIM, :] += _dot(vt, a.astype(MXU_DTYPE))
            carry_ref[h:h + 1, :] = c + jnp.sum(lf, axis=0, keepdims=True)
        return j - 1, jnp.max(carry_ref[...])

    lax.while_loop(lambda s: (s[0] >= 0) & (s[1] >= SB_UNDERFLOW), body,
                   (2 * i + 1, jnp.float32(0.0)))
    g = gate_ref[...].astype(F32)
    o_ref[...] = (acc_ref[...].T * _silu(g)).astype(o_ref.dtype)


def _sb_prompt(qT, k, vT, gate, *, batch, seq):
    nq = seq // TQ
    return pl.pallas_call(
        _sb_prompt_kernel, grid=(batch, nq),
        in_specs=[pl.BlockSpec((None, W_HEADS, TQ), lambda b, i: (b * nq + i, 0, 0)),
                  pl.BlockSpec((seq, W_HEADS), lambda b, i: (b, 0)),
                  pl.BlockSpec((seq // CK, W_HEADS, CK), lambda b, i: (b, 0, 0)),
                  pl.BlockSpec((TQ, W_HEADS), lambda b, i: (b * nq + i, 0))],
        out_specs=pl.BlockSpec((TQ, W_HEADS), lambda b, i: (b * nq + i, 0)),
        out_shape=jax.ShapeDtypeStruct((batch * seq, W_HEADS), MXU_DTYPE),
        scratch_shapes=[pltpu.VMEM((N_HEADS, LANES, TQ), MXU_DTYPE), pltpu.VMEM((N_HEADS, TQ), F32),
                        pltpu.VMEM((W_HEADS, TQ), F32)],
        compiler_params=_cparams(2), name="sb_prompt",
    )(qT, k, vT, gate)


def _monotone_key(score):
    bits = pltpu.bitcast(score, jnp.int32)
    key = bits ^ (lax.shift_right_arithmetic(bits, 31) & 0x7FFFFFFF)
    return jnp.where(score == 0.0, 0, key)


def _topk_bias(load_keys, n_chunks, key_index, count_axis, vec_shape, topk, store_bias, index_bits):
    imin = jnp.int32(INT_MIN)

    def count(pred):
        def step(c, acc):
            return acc + jnp.sum(jnp.where(pred(c), 1.0, 0.0), axis=count_axis, keepdims=True)
        return lax.fori_loop(0, n_chunks, step, jnp.zeros(vec_shape, F32))

    kf = jnp.float32(topk)
    zero = jnp.zeros(vec_shape, jnp.int32)
    t0 = jnp.where(count(lambda c: load_keys(c) >= zero) >= kf, zero, zero + imin)

    def bit_step(bi, t):
        cand = t | lax.shift_left(jnp.int32(1), 30 - bi)
        return jnp.where(count(lambda c: load_keys(c) >= cand) >= kf, cand, t)

    t = lax.fori_loop(0, 31, bit_step, t0)
    n_ge = count(lambda c: load_keys(c) >= t)
    has_tie = jnp.max(jnp.where((n_ge > kf) & (t > imin), 1.0, 0.0)) > 0.0

    def plain():
        t_eff = jnp.maximum(t, imin + 1)

        def step(c, _):
            store_bias(c, jnp.where(load_keys(c) >= t_eff, 0.0, NEG_BIG))
            return 0
        lax.fori_loop(0, n_chunks, step, 0)

    def with_ties():
        need = kf - count(lambda c: load_keys(c) > t)
        p = jnp.zeros(vec_shape, jnp.int32)
        for b in range(index_bits - 1, -1, -1):
            cand = p | (1 << b)
            n = count(lambda c: (load_keys(c) == t) & (key_index(c) < cand))
            p = jnp.where(n < need, cand, p)

        def step(c, _):
            k = load_keys(c)
            sel = ((k > t) | ((k == t) & (key_index(c) <= p))) & (k > imin)
            store_bias(c, jnp.where(sel, 0.0, NEG_BIG))
            return 0
        lax.fori_loop(0, n_chunks, step, 0)

    lax.cond(has_tie, with_ties, plain)


def _dsa_prompt_kernel(qT_ref, k_ref, vT_ref, qiT_ref, ki_ref, wT_ref, gate_ref, o_ref,
                       qz_ref, keys_ref, bias_ref, acc_ref, *, topk, index_bits):
    i = pl.program_id(1)
    q0 = i * TQ
    n_chunks = 2 * i + 2
    _split_head_pairs(qT_ref, qz_ref)
    r_io = lax.broadcasted_iota(jnp.int32, (CK, TQ), 0)
    c_io = lax.broadcasted_iota(jnp.int32, (CK, TQ), 1)
    w = wT_ref[...]

    def score_step(c, _):
        ki = _key_chunk(ki_ref, c, 0, IDX_DIM)
        sc = jnp.zeros((CK, TQ), F32)
        for h in range(N_HEADS):
            d = _dot(ki, qiT_ref[h * IDX_DIM:(h + 1) * IDX_DIM, :])
            sc = sc + w[h:h + 1, :] * jnp.maximum(d, 0.0)
        visible = (c * CK + r_io) <= (q0 + c_io)
        keys_ref[pl.ds(pl.multiple_of(c * CK, CK), CK), :] = jnp.where(visible, _monotone_key(sc), INT_MIN)
        return 0

    lax.fori_loop(0, n_chunks, score_step, 0)

    def load_keys(c):
        return keys_ref[pl.ds(pl.multiple_of(c * CK, CK), CK), :]

    def store_bias(c, b):
        bias_ref[pl.ds(pl.multiple_of(c * CK, CK), CK), :] = b

    _topk_bias(load_keys, n_chunks, lambda c: c * CK + r_io, 0, (1, TQ), topk, store_bias, index_bits)

    for h in range(N_HEADS):
        def att_step(c, st, h=h):
            m, l, acc = st
            kp = _key_chunk(k_ref, c, (h // 2) * LANES, LANES)
            s = _dot(kp, qz_ref[h]) + bias_ref[pl.ds(pl.multiple_of(c * CK, CK), CK), :]
            m_new = jnp.maximum(m, jnp.max(s, axis=0, keepdims=True))
            alpha = jnp.exp(m - m_new)
            p = jnp.exp(s - m_new)
            l = alpha * l + jnp.sum(p, axis=0, keepdims=True)
            vt = vT_ref[c, h * HEAD_DIM:(h + 1) * HEAD_DIM, :]
            acc = alpha * acc + _dot(vt, p.astype(MXU_DTYPE))
            return m_new, l, acc

        m, l, acc = lax.fori_loop(
            0, n_chunks, att_step,
            (jnp.full((1, TQ), NEG_BIG, F32), jnp.zeros((1, TQ), F32), jnp.zeros((HEAD_DIM, TQ), F32)))
        acc_ref[h * HEAD_DIM:(h + 1) * HEAD_DIM, :] = acc / l
    g = gate_ref[...].astype(F32)
    o_ref[...] = (acc_ref[...].T * _silu(g)).astype(o_ref.dtype)


def _dsa_prompt(qT, k, vT, qiT, ki, wT, gate, *, batch, seq):
    nq = seq // TQ
    topk = min(TOPK_MAX, seq // 4)
    qspec = pl.BlockSpec((None, W_HEADS, TQ), lambda b, i: (b * nq + i, 0, 0))
    return pl.pallas_call(
        functools.partial(_dsa_prompt_kernel, topk=topk, index_bits=max(1, (seq - 1).bit_length())),
        grid=(batch, nq),
        in_specs=[qspec,
                  pl.BlockSpec((seq, W_HEADS), lambda b, i: (b, 0)),
                  pl.BlockSpec((seq // CK, W_HEADS, CK), lambda b, i: (b, 0, 0)),
                  qspec,
                  pl.BlockSpec((seq, IDX_DIM), lambda b, i: (b, 0)),
                  pl.BlockSpec((None, N_HEADS, TQ), lambda b, i: (b * nq + i, 0, 0)),
                  pl.BlockSpec((TQ, W_HEADS), lambda b, i: (b * nq + i, 1))],
        out_specs=pl.BlockSpec((TQ, W_HEADS), lambda b, i: (b * nq + i, 0)),
        out_shape=jax.ShapeDtypeStruct((batch * seq, W_HEADS), MXU_DTYPE),
        scratch_shapes=[pltpu.VMEM((N_HEADS, LANES, TQ), MXU_DTYPE), pltpu.VMEM((seq, TQ), jnp.int32),
                        pltpu.VMEM((seq, TQ), F32), pltpu.VMEM((W_HEADS, TQ), F32)],
        compiler_params=_cparams(2), name="dsa_prompt",
    )(qT, k, vT, qiT, ki, wT, gate)


def _diff_lambda(lam_ref, lambda_init):
    lp = lam_ref[...]
    s1 = jnp.sum(lp[0:1, :] * lp[1:2, :], axis=-1, keepdims=True)
    s2 = jnp.sum(lp[2:3, :] * lp[3:4, :], axis=-1, keepdims=True)
    return jnp.exp(s1) - jnp.exp(s2) + lambda_init


def _diff_prompt_kernel(lam_ref, qT_ref, k_ref, vT_ref, gate_ref, g_ref, o_ref,
                        qz_ref, acc_ref, out_ref, *, lambda_init):
    i = pl.program_id(1)
    q0 = i * TQ
    n_chunks = 2 * i + 2
    _split_head_pairs(qT_ref, qz_ref)
    lam = _diff_lambda(lam_ref, lambda_init)
    r_io = lax.broadcasted_iota(jnp.int32, (CK, TQ), 0)
    c_io = lax.broadcasted_iota(jnp.int32, (CK, TQ), 1)
    width = 2 * HEAD_DIM

    for h in range(N_HEADS):
        acc_ref[...] = jnp.zeros_like(acc_ref)

        def att_step(c, st, h=h):
            kp = _key_chunk(k_ref, c, h * width, width)
            vt = vT_ref[c, h * width:(h + 1) * width, :]
            bias = jnp.where((c * CK + r_io) <= (q0 + c_io), 0.0, NEG_BIG)
            new = []
            for comp in range(2):
                m, l = st[2 * comp], st[2 * comp + 1]
                s = _dot(kp, qz_ref[2 * h + comp]) + bias
                m_new = jnp.maximum(m, jnp.max(s, axis=0, keepdims=True))
                alpha = jnp.exp(m - m_new)
                p = jnp.exp(s - m_new)
                l = alpha * l + jnp.sum(p, axis=0, keepdims=True)
                acc_ref[comp] = alpha * acc_ref[comp] + _dot(vt, p.astype(MXU_DTYPE))
                new += [m_new, l]
            return tuple(new)

        init = (jnp.full((1, TQ), NEG_BIG, F32), jnp.zeros((1, TQ), F32)) * 2
        _, l0, _, l1 = lax.fori_loop(0, n_chunks, att_step, init)
        o = acc_ref[0] / l0 - lam * (acc_ref[1] / l1)
        ms = jnp.mean(o * o, axis=0, keepdims=True)
        out_ref[h * width:(h + 1) * width, :] = o * lax.rsqrt(ms + SUBLN_EPS)
    g = gate_ref[...].astype(F32)
    o_ref[...] = (out_ref[...].T * g_ref[...] * (1.0 - lambda_init) * _silu(g)).astype(o_ref.dtype)


def _diff_prompt(lam_params, qT, k, vT, gate, g_tiled, *, batch, seq, lambda_init):
    nq = seq // TQ
    return pl.pallas_call(
        functools.partial(_diff_prompt_kernel, lambda_init=lambda_init), grid=(batch, nq),
        in_specs=[pl.BlockSpec((4, HEAD_DIM), lambda b, i: (0, 0)),
                  pl.BlockSpec((None, W_DIFF, TQ), lambda b, i: (b * nq + i, 0, 0)),
                  pl.BlockSpec((seq, W_DIFF), lambda b, i: (b, 0)),
                  pl.BlockSpec((seq // CK, W_DIFF, CK), lambda b, i: (b, 0, 0)),
                  pl.BlockSpec((TQ, W_DIFF), lambda b, i: (b * nq + i, 0)),
                  pl.BlockSpec((1, W_DIFF), lambda b, i: (0, 0))],
        out_specs=pl.BlockSpec((TQ, W_DIFF), lambda b, i: (b * nq + i, 0)),
        out_shape=jax.ShapeDtypeStruct((batch * seq, W_DIFF), MXU_DTYPE),
        scratch_shapes=[pltpu.VMEM((2 * N_HEADS, LANES, TQ), MXU_DTYPE),
                        pltpu.VMEM((2, 2 * HEAD_DIM, TQ), F32), pltpu.VMEM((W_DIFF, TQ), F32)],
        compiler_params=_cparams(2), name="diff_prompt",
    )(lam_params, qT, k, vT, gate, g_tiled)


def _block_diag_q(q, n_blocks):
    db, t, w = q.shape
    q5 = q.reshape(db, t, 1, n_blocks, w // n_blocks)
    eye = jnp.eye(n_blocks, dtype=q.dtype).reshape(1, 1, n_blocks, n_blocks, 1)
    return (q5 * eye).reshape(db, t * n_blocks, w)


def _paged_specs(n_pages, page, width, layer):
    return [pl.BlockSpec((None, None, page, width),
                         lambda b, pt, p=p: (layer, pt[b * n_pages + p], 0, 0)) for p in range(n_pages)]


def _new_tile(new_ref, pad_ref):
    @pl.when(pl.program_id(0) == 0)
    def _():
        pad_ref[...] = jnp.zeros_like(pad_ref)
    pad_ref[0:new_ref.shape[0], :] = new_ref[...]
    return pad_ref[...]


def _rows_token(shape):
    return lax.shift_right_logical(lax.broadcasted_iota(jnp.int32, shape, 0), N_HEADS.bit_length() - 1)


def _head_block_sum(acc, n_tokens, block):
    rows, cols = acc.shape
    r = lax.broadcasted_iota(jnp.int32, acc.shape, 0)
    c = lax.broadcasted_iota(jnp.int32, acc.shape, 1)
    c_block = lax.shift_right_logical(c, block.bit_length() - 1)
    return jnp.where(c_block == (r & (N_HEADS - 1)), acc, 0.0), (n_tokens, N_HEADS, cols)


def _sb_sample_kernel(pt_ref, qbd_ref, *refs, n_pages, page, n_new):
    page_refs = refs[:n_pages]
    new_ref, gate_ref, o_ref, pad_ref = refs[n_pages:]
    qbd = qbd_ref[...]
    rows = qbd.shape[0]
    u_r = lax.broadcasted_iota(jnp.int32, (2 * page, page), 0)
    u_c = lax.broadcasted_iota(jnp.int32, (2 * page, page), 1)
    u2 = jnp.where((u_r & (page - 1)) > u_c, 1.0, 0.0).astype(MXU_DTYPE)
    lane = lax.broadcasted_iota(jnp.int32, (rows, page), 1)
    new_before = lane < _rows_token((rows, page))

    def tile(kv, mask, carry, acc):
        k = kv[:, :W_HEADS].astype(MXU_DTYPE)
        v = kv[:, W_HEADS:].astype(MXU_DTYPE)
        z = _dot_nt(qbd, k)
        ls_pos = jnp.minimum(z, 0.0) - jnp.log(1.0 + jnp.exp(-jnp.abs(z)))
        lf = ls_pos - z
        if mask is not None:
            lf = jnp.where(mask, lf, 0.0)
        hi = lf.astype(MXU_DTYPE)
        lo = (lf - hi.astype(F32)).astype(MXU_DTYPE)
        later = _dot(jnp.concatenate([hi, lo], axis=1), u2)
        a = jnp.exp(ls_pos + later + carry)
        if mask is not None:
            a = jnp.where(mask, a, 0.0)
        acc = acc + _dot(a.astype(MXU_DTYPE), v)
        return carry + jnp.sum(lf, axis=1, keepdims=True), acc

    carry = jnp.zeros((rows, 1), F32)
    acc = jnp.zeros((rows, W_HEADS), F32)
    carry, acc = tile(_new_tile(new_ref, pad_ref), new_before, carry, acc)
    for p in range(n_pages - 1, -1, -1):
        carry, acc = tile(page_refs[p][...], None, carry, acc)
    masked, shape3 = _head_block_sum(acc, n_new, HEAD_DIM)
    o = jnp.sum(masked.reshape(shape3), axis=1)
    g = gate_ref[...].astype(F32)
    o_ref[...] = (o * _silu(g)).astype(o_ref.dtype)


def _sb_sample(page_table, qbd, cache, layer, new_kv, gate):
    db, rows, _ = qbd.shape
    n_pages = page_table.shape[1]
    page = cache.shape[2]
    n_new = new_kv.shape[1]
    cache4 = cache.reshape(cache.shape[0], cache.shape[1], page, 2 * W_HEADS)
    grid_spec = pltpu.PrefetchScalarGridSpec(
        num_scalar_prefetch=1, grid=(db,),
        in_specs=[pl.BlockSpec((None, rows, W_HEADS), lambda b, pt: (b, 0, 0))]
        + _paged_specs(n_pages, page, 2 * W_HEADS, layer)
        + [pl.BlockSpec((None, n_new, 2 * W_HEADS), lambda b, pt: (b, 0, 0)),
           pl.BlockSpec((None, n_new, W_HEADS), lambda b, pt: (b, 0, 0))],
        out_specs=pl.BlockSpec((None, n_new, W_HEADS), lambda b, pt: (b, 0, 0)),
        scratch_shapes=[pltpu.VMEM((page, 2 * W_HEADS), F32)])
    return pl.pallas_call(
        functools.partial(_sb_sample_kernel, n_pages=n_pages, page=page, n_new=n_new),
        grid_spec=grid_spec, out_shape=jax.ShapeDtypeStruct((db, n_new, W_HEADS), MXU_DTYPE),
        compiler_params=_cparams(1), name="sb_sample",
    )(page_table.reshape(-1), qbd, *([cache4] * n_pages), new_kv, gate)


def _idx_sample_kernel(pt_ref, qi_ref, w_ref, *refs, n_pages, page, n_new):
    page_refs = refs[:n_pages]
    new_ref, o_ref, pad_ref = refs[n_pages:]
    qi = qi_ref[...]
    w = w_ref[...]
    rows = qi.shape[0]

    def tile(kidx):
        d = _dot_nt(qi, kidx.astype(MXU_DTYPE))
        r = w * jnp.maximum(d, 0.0)
        return _monotone_key(jnp.sum(r.reshape(n_new, N_HEADS, page), axis=1))

    for p in range(n_pages):
        o_ref[:, p * page:(p + 1) * page] = tile(page_refs[p][...])
    lane = lax.broadcasted_iota(jnp.int32, (n_new, page), 1)
    tok = lax.broadcasted_iota(jnp.int32, (n_new, page), 0)
    new_keys = tile(_new_tile(new_ref, pad_ref))
    o_ref[:, n_pages * page:] = jnp.where(lane <= tok, new_keys, INT_MIN)


def _idx_sample(page_table, qi, w_rep, cache, layer, new_ki):
    db, rows, _ = qi.shape
    n_pages = page_table.shape[1]
    page = cache.shape[2]
    n_new = new_ki.shape[1]
    grid_spec = pltpu.PrefetchScalarGridSpec(
        num_scalar_prefetch=1, grid=(db,),
        in_specs=[pl.BlockSpec((None, rows, IDX_DIM), lambda b, pt: (b, 0, 0)),
                  pl.BlockSpec((None, rows, page), lambda b, pt: (b, 0, 0))]
        + _paged_specs(n_pages, page, IDX_DIM, layer)
        + [pl.BlockSpec((None, n_new, IDX_DIM), lambda b, pt: (b, 0, 0))],
        out_specs=pl.BlockSpec((None, n_new, (n_pages + 1) * page), lambda b, pt: (b, 0, 0)),
        scratch_shapes=[pltpu.VMEM((page, IDX_DIM), F32)])
    return pl.pallas_call(
        functools.partial(_idx_sample_kernel, n_pages=n_pages, page=page, n_new=n_new),
        grid_spec=grid_spec,
        out_shape=jax.ShapeDtypeStruct((db, n_new, (n_pages + 1) * page), jnp.int32),
        compiler_params=_cparams(1), name="idx_sample",
    )(page_table.reshape(-1), qi, w_rep, *([cache] * n_pages), new_ki)


def _topk_sample_kernel(keys_ref, bias_ref, *, topk, index_bits):
    rows, n = keys_ref.shape
    lane = lax.broadcasted_iota(jnp.int32, (rows, n), 1)

    def store_bias(c, b):
        bias_ref[...] = b

    _topk_bias(lambda c: keys_ref[...], 1, lambda c: lane, 1, (rows, 1), topk, store_bias, index_bits)


def _topk_sample(keys, topk):
    r, n = keys.shape
    tm = min(r, 64)
    return pl.pallas_call(
        functools.partial(_topk_sample_kernel, topk=topk, index_bits=max(1, (n - 1).bit_length())),
        grid=(r // tm,), in_specs=[pl.BlockSpec((tm, n), lambda i: (i, 0))],
        out_specs=pl.BlockSpec((tm, n), lambda i: (i, 0)), out_shape=jax.ShapeDtypeStruct((r, n), F32),
        compiler_params=_cparams(1), name="topk_sample",
    )(keys)


def _dsa_sample_kernel(pt_ref, qbd_ref, *refs, n_pages, page, n_new):
    page_refs = refs[:n_pages]
    new_ref, bias_ref, gate_ref, o_ref, pad_ref, s_ref = refs[n_pages:]
    qbd = qbd_ref[...]
    rows = qbd.shape[0]
    new_kv = _new_tile(new_ref, pad_ref)

    def kv_tile(p):
        return page_refs[p][...] if p < n_pages else new_kv

    for p in range(n_pages + 1):
        k = kv_tile(p)[:, :W_HEADS].astype(MXU_DTYPE)
        s_ref[:, p * page:(p + 1) * page] = _dot_nt(qbd, k) + bias_ref[:, p * page:(p + 1) * page]
    s = s_ref[...]
    m = jnp.max(s, axis=1, keepdims=True)
    e = jnp.exp(s - m)
    l = jnp.sum(e, axis=1, keepdims=True)
    s_ref[...] = e
    acc = jnp.zeros((rows, W_HEADS), F32)
    for p in range(n_pages + 1):
        v = kv_tile(p)[:, W_HEADS:].astype(MXU_DTYPE)
        acc = acc + _dot(s_ref[:, p * page:(p + 1) * page].astype(MXU_DTYPE), v)
    masked, shape3 = _head_block_sum(acc / l, n_new, HEAD_DIM)
    o = jnp.sum(masked.reshape(shape3), axis=1)
    g = gate_ref[...].astype(F32)
    o_ref[...] = (o * _silu(g)).astype(o_ref.dtype)


def _dsa_sample(page_table, qbd, cache, layer, new_kv, bias, gate):
    db, rows, _ = qbd.shape
    n_pages = page_table.shape[1]
    page = cache.shape[2]
    n_new = new_kv.shape[1]
    n_keys = (n_pages + 1) * page
    cache4 = cache.reshape(cache.shape[0], cache.shape[1], page, 2 * W_HEADS)
    grid_spec = pltpu.PrefetchScalarGridSpec(
        num_scalar_prefetch=1, grid=(db,),
        in_specs=[pl.BlockSpec((None, rows, W_HEADS), lambda b, pt: (b, 0, 0))]
        + _paged_specs(n_pages, page, 2 * W_HEADS, layer)
        + [pl.BlockSpec((None, n_new, 2 * W_HEADS), lambda b, pt: (b, 0, 0)),
           pl.BlockSpec((None, rows, n_keys), lambda b, pt: (b, 0, 0)),
           pl.BlockSpec((None, n_new, W_HEADS), lambda b, pt: (b, 0, 1))],
        out_specs=pl.BlockSpec((None, n_new, W_HEADS), lambda b, pt: (b, 0, 0)),
        scratch_shapes=[pltpu.VMEM((page, 2 * W_HEADS), F32), pltpu.VMEM((rows, n_keys), F32)])
    return pl.pallas_call(
        functools.partial(_dsa_sample_kernel, n_pages=n_pages, page=page, n_new=n_new),
        grid_spec=grid_spec, out_shape=jax.ShapeDtypeStruct((db, n_new, W_HEADS), MXU_DTYPE),
        compiler_params=_cparams(1), name="dsa_sample",
    )(page_table.reshape(-1), qbd, *([cache4] * n_pages), new_kv, bias, gate)


def _diff_sample_kernel(pt_ref, lam_ref, qbd_ref, *refs, n_pages, page, n_new, lambda_init):
    page_refs = refs[:n_pages]
    new_ref, gate_ref, g_ref, o_ref, pad_ref, s_ref = refs[n_pages:]
    qbd = qbd_ref[...]
    rows = qbd.shape[0]
    half = rows // 2
    lam = _diff_lambda(lam_ref, lambda_init)
    new_kv = _new_tile(new_ref, pad_ref)

    def kv_tile(p):
        return page_refs[p][...] if p < n_pages else new_kv

    for p in range(n_pages + 1):
        k = kv_tile(p)[:, :W_DIFF].astype(MXU_DTYPE)
        s = _dot_nt(qbd, k)
        if p == n_pages:
            lane = lax.broadcasted_iota(jnp.int32, (rows, page), 1)
            tok = _rows_token((rows, page)) & (n_new - 1)
            s = jnp.where(lane <= tok, s, NEG_BIG)
        s_ref[:, p * page:(p + 1) * page] = s
    s = s_ref[...]
    m = jnp.max(s, axis=1, keepdims=True)
    e = jnp.exp(s - m)
    pr = e / jnp.sum(e, axis=1, keepdims=True)
    s_ref[0:half, :] = pr[:half] - lam * pr[half:]
    acc = jnp.zeros((half, W_DIFF), F32)
    for p in range(n_pages + 1):
        v = kv_tile(p)[:, W_DIFF:].astype(MXU_DTYPE)
        acc = acc + _dot(s_ref[0:half, p * page:(p + 1) * page].astype(MXU_DTYPE), v)
    masked, shape3 = _head_block_sum(acc, n_new, 2 * HEAD_DIM)
    ms = jnp.sum(masked * masked, axis=1, keepdims=True) / (2 * HEAD_DIM)
    o = jnp.sum((masked * lax.rsqrt(ms + SUBLN_EPS)).reshape(shape3), axis=1)
    g = gate_ref[...].astype(F32)
    o_ref[...] = (o * g_ref[...] * (1.0 - lambda_init) * _silu(g)).astype(o_ref.dtype)


def _diff_sample(page_table, lam_params, qbd, cache, layer, new_kv, gate, g_tiled, *, lambda_init):
    db, rows, _ = qbd.shape
    n_pages = page_table.shape[1]
    page = cache.shape[2]
    n_new = new_kv.shape[1]
    n_keys = (n_pages + 1) * page
    cache4 = cache.reshape(cache.shape[0], cache.shape[1], page, 2 * W_DIFF)
    grid_spec = pltpu.PrefetchScalarGridSpec(
        num_scalar_prefetch=1, grid=(db,),
        in_specs=[pl.BlockSpec((4, HEAD_DIM), lambda b, pt: (0, 0)),
                  pl.BlockSpec((None, rows, W_DIFF), lambda b, pt: (b, 0, 0))]
        + _paged_specs(n_pages, page, 2 * W_DIFF, layer)
        + [pl.BlockSpec((None, n_new, 2 * W_DIFF), lambda b, pt: (b, 0, 0)),
           pl.BlockSpec((None, n_new, W_DIFF), lambda b, pt: (b, 0, 0)),
           pl.BlockSpec((1, W_DIFF), lambda b, pt: (0, 0))],
        out_specs=pl.BlockSpec((None, n_new, W_DIFF), lambda b, pt: (b, 0, 0)),
        scratch_shapes=[pltpu.VMEM((page, 2 * W_DIFF), F32), pltpu.VMEM((rows, n_keys), F32)])
    return pl.pallas_call(
        functools.partial(_diff_sample_kernel, n_pages=n_pages, page=page, n_new=n_new,
                          lambda_init=lambda_init),
        grid_spec=grid_spec, out_shape=jax.ShapeDtypeStruct((db, n_new, W_DIFF), MXU_DTYPE),
        compiler_params=_cparams(1), name="diff_sample",
    )(page_table.reshape(-1), lam_params, qbd, *([cache4] * n_pages), new_kv, gate, g_tiled)


def _ab_layer_prompt(x, g, w_in, w_out, tables, batch, seq):
    (kv_sb, kv_d, misc, qT_sb, k_sb, vT_sb, qT_d, k_d, vT_d, qT_i, k_i, wT, gate) = _proj_ab(
        x, g, w_in, tables, rows_per_seq=seq, transposed=True)
    og_sb = _sb_prompt(qT_sb, k_sb, vT_sb, gate, batch=batch, seq=seq)
    og_d = _dsa_prompt(qT_d, k_d, vT_d, qT_i, k_i, wT, gate, batch=batch, seq=seq)
    return _out_ab(x, og_sb, og_d, w_out), kv_sb, kv_d, misc[:, :IDX_DIM]


def _ab_layer_sample(x, g, w_in, w_out, tables, db, n_new, page_table, layer,
                     cache_kv_sb, cache_kv_dsa, cache_kidx):
    kv_sb, kv_d, misc, q_sb, q_d, q_i, gate = _proj_ab(
        x, g, w_in, tables, rows_per_seq=x.shape[0], transposed=False)
    page = cache_kv_sb.shape[2]
    past = page_table.shape[1] * page
    gate3 = gate.reshape(db, n_new, 2 * W_HEADS)
    kv_sb3 = kv_sb.reshape(db, n_new, 2 * W_HEADS)
    kv_d3 = kv_d.reshape(db, n_new, 2 * W_HEADS)
    misc3 = misc.reshape(db, n_new, LANES)
    og_sb = _sb_sample(page_table, _block_diag_q(q_sb.reshape(db, n_new, W_HEADS), N_HEADS),
                       cache_kv_sb, layer, kv_sb3, gate3)
    w_rep = jnp.broadcast_to(misc3[:, :, IDX_DIM:IDX_DIM + N_HEADS].reshape(db, n_new * N_HEADS, 1),
                             (db, n_new * N_HEADS, page))
    keys = _idx_sample(page_table, q_i.reshape(db, n_new * N_HEADS, IDX_DIM), w_rep, cache_kidx, layer,
                       misc3[:, :, :IDX_DIM])
    n_keys = keys.shape[-1]
    topk = min(TOPK_MAX, (past + n_new) // 4)
    bias = _topk_sample(keys.reshape(db * n_new, n_keys), topk).reshape(db, n_new, 1, n_keys)
    bias = jnp.broadcast_to(bias, (db, n_new, N_HEADS, n_keys)).reshape(db, n_new * N_HEADS, n_keys)
    og_d = _dsa_sample(page_table, _block_diag_q(q_d.reshape(db, n_new, W_HEADS), N_HEADS),
                       cache_kv_dsa, layer, kv_d3, bias, gate3)
    out = _out_ab(x, og_sb.reshape(db * n_new, W_HEADS), og_d.reshape(db * n_new, W_HEADS), w_out)
    return out, kv_sb, kv_d, misc[:, :IDX_DIM]


def kernel(x_prompt, x_sample, cache_kv_sb, cache_kv_dsa, cache_kidx_dsa, cache_kv_diff, page_table,
           norm_g, w_in_ab, w_out_ab, w_in_c, w_out_c, diff_lambda, diff_subln_g, final_norm_g):
    batch, seq, d = x_prompt.shape
    db, n_new, _ = x_sample.shape
    depth = norm_g.shape[0]
    page = cache_kv_sb.shape[2]
    past = page_table.shape[1] * page
    tab_p = _rope_tables(jnp.arange(seq))
    tab_s = _rope_tables(jnp.tile(past + jnp.arange(n_new), db))
    xp = x_prompt.reshape(batch * seq, d)
    xs = x_sample.reshape(db * n_new, d)
    sb_p, sb_s, dsa_p, dsa_s, idx_p, idx_s, dif_p, dif_s = [], [], [], [], [], [], [], []
    fin = final_norm_g.reshape(1, d)
    for layer in range(depth):
        li = layer // 2
        g = norm_g[layer].reshape(1, d)
        last = layer == depth - 1
        if layer % 2 == 0:
            w_in = _pack_w_in_ab(w_in_ab[li])
            w_out = w_out_ab[li].astype(MXU_DTYPE)
            xp, kv_sb, kv_d, ki = _ab_layer_prompt(xp, g, w_in, w_out, tab_p, batch, seq)
            sb_p.append(kv_sb.reshape(batch, seq, 2, N_HEADS, HEAD_DIM))
            dsa_p.append(kv_d.reshape(batch, seq, 2, N_HEADS, HEAD_DIM))
            idx_p.append(ki.reshape(batch, seq, IDX_DIM))
            xs, kv_sb, kv_d, ki = _ab_layer_sample(xs, g, w_in, w_out, tab_s, db, n_new, page_table, li,
                                                   cache_kv_sb, cache_kv_dsa, cache_kidx_dsa)
            sb_s.append(kv_sb.reshape(db, n_new, 2, N_HEADS, HEAD_DIM))
            dsa_s.append(kv_d.reshape(db, n_new, 2, N_HEADS, HEAD_DIM))
            idx_s.append(ki.reshape(db, n_new, IDX_DIM))
            if last:
                xp, xs = _final_norm(xp, fin), _final_norm(xs, fin)
        else:
            lambda_init = 0.8 - 0.6 * math.exp(-0.3 * layer)
            w_in = w_in_c[li].astype(MXU_DTYPE)
            w_out = w_out_c[li].astype(MXU_DTYPE)
            g_tiled = jnp.tile(diff_subln_g[li], N_HEADS).reshape(1, W_DIFF)
            kv, qT, k, vT, gate = _proj_c(xp, g, w_in, tab_p, rows_per_seq=seq, transposed=True)
            og = _diff_prompt(diff_lambda[li], qT, k, vT, gate, g_tiled, batch=batch, seq=seq,
                              lambda_init=lambda_init)
            xp = _out_c(xp, og, w_out, fin, final_norm=last)
            dif_p.append(kv.reshape(batch, seq, 2, N_HEADS, 2 * HEAD_DIM))
            kv, q, gate = _proj_c(xs, g, w_in, tab_s, rows_per_seq=xs.shape[0], transposed=False)
            qbd = _block_diag_q(q.reshape(db, n_new, W_DIFF), 2 * N_HEADS)
            qbd = qbd.reshape(db, n_new, N_HEADS, 2, W_DIFF).transpose(0, 3, 1, 2, 4).reshape(
                db, 2 * n_new * N_HEADS, W_DIFF)
            og = _diff_sample(page_table, diff_lambda[li], qbd, cache_kv_diff, li,
                              kv.reshape(db, n_new, 2 * W_DIFF), gate.reshape(db, n_new, W_DIFF), g_tiled,
                              lambda_init=lambda_init)
            xs = _out_c(xs, og.reshape(db * n_new, W_DIFF), w_out, fin, final_norm=last)
            dif_s.append(kv.reshape(db, n_new, 2, N_HEADS, 2 * HEAD_DIM))
    return (xp.reshape(batch, seq, d), xs.reshape(db, n_new, d),
            jnp.stack(sb_p), jnp.stack(sb_s), jnp.stack(dsa_p), jnp.stack(dsa_s),
            jnp.stack(idx_p), jnp.stack(idx_s), jnp.stack(dif_p), jnp.stack(dif_s))
```
